```python
import math
import jax, jax.numpy as jnp
from jax import lax
import numpy as np

D_MODEL = 1024
BATCH = 8
SEQ = 8192
DEPTH = 2
DEC_BATCH = 32
DEC_SEQ = 64
PAST_LEN = 1024

CHUNK = 64
N_META = 16
N_MIXERS = 2
N_ATTN = (DEPTH + 1) // 2
N_POOL = DEPTH // 2
N_HEADS = 8
HEAD_DIM = 64
V_DIM = 2 * HEAD_DIM
QK_WIDTH = N_HEADS * 2 * HEAD_DIM
V_WIDTH = N_HEADS * V_DIM
ROPE_THETA = 10000.0
Q_BLOCK = 128
POOL_WINDOWS = (2, 4, 8, 16)
N_POOL_GROUPS = 4
POOL_GROUP = D_MODEL // N_POOL_GROUPS
POOL_STATE = max(POOL_WINDOWS) - 1
PEER_HEADS = 8
N_KEYS = 128
N_EXPERTS = N_KEYS * N_KEYS
D_KEY = 256
D_HALF = D_KEY // 2
PEER_TOPK = 16
PEER_BLOCK = 256
EPS = 1e-6
NEG = -1e30

kernel_name = "streaming_diffattn_pool_peer_step"


def _rmsnorm(x, g):
    xf = x.astype(jnp.float32)
    y = xf * lax.rsqrt(jnp.mean(xf * xf, axis=-1, keepdims=True) + EPS)
    return (y * g.astype(jnp.float32)).astype(x.dtype)


def _rope(x, pos):
    half = HEAD_DIM // 2
    inv = ROPE_THETA ** (-jnp.arange(half, dtype=jnp.float32) / half)
    ang = pos.astype(jnp.float32)[:, None] * inv[None, :]
    cos = jnp.cos(ang)[:, None, :]
    sin = jnp.sin(ang)[:, None, :]
    xf = x.astype(jnp.float32)
    x1, x2 = xf[..., :half], xf[..., half:]
    return jnp.concatenate([x1 * cos - x2 * sin, x2 * cos + x1 * sin], axis=-1).astype(x.dtype)


def _chunk_id(pos):
    return jnp.where(pos < N_META, -1, (pos - N_META) // CHUNK)


def _diff_lambda(lq1, lk1, lq2, lk2, lam_init):
    f = jnp.float32
    return (jnp.exp(jnp.sum(lq1.astype(f) * lk1.astype(f)))
            - jnp.exp(jnp.sum(lq2.astype(f) * lk2.astype(f))) + lam_init)


def _diff_qkv(h, w_qkv, pos):
    B, L, _ = h.shape
    qkv = h @ w_qkv
    q = qkv[..., :QK_WIDTH].reshape(B, L, 2 * N_HEADS, HEAD_DIM)
    k = qkv[..., QK_WIDTH:2 * QK_WIDTH].reshape(B, L, 2 * N_HEADS, HEAD_DIM)
    v = qkv[..., 2 * QK_WIDTH:].reshape(B, L, N_HEADS, V_DIM)
    q = _rope(q, pos).reshape(B, L, N_HEADS, 2, HEAD_DIM)
    k = _rope(k, pos).reshape(B, L, N_HEADS, 2 * HEAD_DIM)
    return q, k, v


def _diff_core(q, k, v, lam, mask):
    B, K = k.shape[0], k.shape[1]
    k = k.reshape(B, K, N_HEADS, 2, HEAD_DIM)
    s = jnp.einsum('bqhmd,bkhmd->bmhqk', q, k).astype(jnp.float32) * (HEAD_DIM ** -0.5)
    if mask is not None:
        s = jnp.where(mask, s, NEG)
    a = jax.nn.softmax(s, axis=-1)
    attn = a[:, 0] - lam * a[:, 1]
    return jnp.einsum('bhqk,bkhe->bqhe', attn.astype(v.dtype), v)


def _diff_out(o, w_o, gain, lam_init):
    B, L = o.shape[0], o.shape[1]
    o = _rmsnorm(o, gain) * (1.0 - lam_init)
    return o.reshape(B, L, V_WIDTH) @ w_o


def _diff_attn_prompt(h, w_qkv, w_o, lam, lam_init, gain):
    B, L, _ = h.shape
    pos = jnp.arange(L)
    q, k, v = _diff_qkv(h, w_qkv, pos)
    nb = -(-L // Q_BLOCK)
    lp = nb * Q_BLOCK
    qp = jnp.pad(q, ((0, 0), (0, lp - L), (0, 0), (0, 0), (0, 0)))
    qb = qp.reshape(B, nb, Q_BLOCK, N_HEADS, 2, HEAD_DIM).transpose(1, 0, 2, 3, 4, 5)
    qpos = jnp.arange(lp).reshape(nb, Q_BLOCK)
    kcid = _chunk_id(pos)

    def block(args):
        qi, pi = args
        mask = kcid[None, :] <= _chunk_id(pi)[:, None]
        return _diff_core(qi, k, v, lam, mask)

    o = lax.map(block, (qb, qpos))
    o = o.transpose(1, 0, 2, 3, 4).reshape(B, lp, N_HEADS, V_DIM)[:, :L]
    return _diff_out(o, w_o, gain, lam_init), k, v


def _diff_attn_sample(h, ck, cv, w_qkv, w_o, lam, lam_init, gain):
    S = h.shape[1]
    pos = ck.shape[1] + jnp.arange(S)
    q, k, v = _diff_qkv(h, w_qkv, pos)
    kk = jnp.concatenate([ck.astype(k.dtype), k], axis=1)
    vv = jnp.concatenate([cv.astype(v.dtype), v], axis=1)
    o = _diff_core(q, kk, vv, lam, None)
    return _diff_out(o, w_o, gain, lam_init), k, v


def _multi_pool(xp, first_pos, w_pool, scale):
    B, T, D = xp.shape
    S = T - POOL_STATE
    xf = xp.astype(jnp.float32)
    cs = jnp.concatenate([jnp.zeros((B, 1, D), jnp.float32), lax.cumsum(xf, axis=1)], axis=1)
    cur = xf[:, POOL_STATE:]
    pos = first_pos + jnp.arange(S)
    outs = []
    for g, w in enumerate(POOL_WINDOWS):
        sl = slice(g * POOL_GROUP, (g + 1) * POOL_GROUP)
        wsum = cs[:, POOL_STATE + 1:, sl] - cs[:, POOL_STATE + 1 - w:T + 1 - w, sl]
        cnt = jnp.minimum(w, pos + 1).astype(jnp.float32)[None, :, None]
        outs.append(wsum / cnt - cur[..., sl])
    d = jnp.stack(outs, axis=2)
    y = jnp.einsum('bsgc,gce->bsge', d, w_pool.astype(jnp.float32)).reshape(B, S, D)
    return (y * scale.astype(jnp.float32)).astype(xp.dtype)


def _peer_block(t, wq, keys, eu, ev):
    T = t.shape[0]
    q = (t @ wq).reshape(T, PEER_HEADS, 2, D_HALF)
    s = jnp.einsum('thpc,hpnc->thpn', q, keys).astype(jnp.float32)
    sv, si = lax.top_k(s, PEER_TOPK)
    comb = (sv[:, :, 0, :, None] + sv[:, :, 1, None, :]).reshape(T, PEER_HEADS, PEER_TOPK * PEER_TOPK)
    cv, ci = lax.top_k(comb, PEER_TOPK)
    i1 = jnp.take_along_axis(si[:, :, 0], ci // PEER_TOPK, axis=-1)
    i2 = jnp.take_along_axis(si[:, :, 1], ci % PEER_TOPK, axis=-1)
    e = i1 * N_KEYS + i2
    g = jax.nn.softmax(cv, axis=-1)
    hid = jnp.einsum('thkd,td->thk', eu[e], t).astype(jnp.float32)
    coef = (g * jax.nn.gelu(hid, approximate=False)).astype(t.dtype)
    return jnp.einsum('thk,thkd->td', coef, ev[e])


def _peer(h, wq, keys, eu, ev):
    B, L, D = h.shape
    n = B * L
    nb = -(-n // PEER_BLOCK)
    t = jnp.pad(h.reshape(n, D), ((0, nb * PEER_BLOCK - n), (0, 0))).reshape(nb, PEER_BLOCK, D)
    y = lax.map(lambda tb: _peer_block(tb, wq, keys, eu, ev), t)
    return y.reshape(nb * PEER_BLOCK, D)[:n].reshape(B, L, D)


def setup_inputs(seed: int = 0) -> dict:
    key = jax.random.key(seed)
    ks = jax.random.split(key, 24)

    def nrm(k, shape, s):
        return jax.random.normal(k, shape, jnp.float32) * s

    return {
        "x_prompt": nrm(ks[0], (BATCH, SEQ, D_MODEL), 1.0),
        "x_sample": nrm(ks[1], (DEC_BATCH, DEC_SEQ, D_MODEL), 1.0),
        "cache_k": nrm(ks[2], (N_ATTN, DEC_BATCH, PAST_LEN, N_HEADS, 2 * HEAD_DIM), 1.0),
        "cache_v": nrm(ks[3], (N_ATTN, DEC_BATCH, PAST_LEN, N_HEADS, V_DIM), 1.0),
        "state_pool": nrm(ks[4], (N_POOL, DEC_BATCH, POOL_STATE, D_MODEL), 1.0),
        "meta_tokens": nrm(ks[5], (N_META, D_MODEL), 1.0),
        "norm_mix": 1.0 + nrm(ks[6], (DEPTH, D_MODEL), 0.05),
        "norm_ffn": 1.0 + nrm(ks[7], (DEPTH, D_MODEL), 0.05),
        "norm_final": 1.0 + nrm(ks[8], (D_MODEL,), 0.05),
        "w_qkv": nrm(ks[9], (N_ATTN, D_MODEL, 2 * QK_WIDTH + V_WIDTH), D_MODEL ** -0.5),
        "w_o": nrm(ks[10], (N_ATTN, V_WIDTH, D_MODEL), V_WIDTH ** -0.5),
        "lambda_q1": nrm(ks[11], (N_ATTN, HEAD_DIM), 0.1),
        "lambda_k1": nrm(ks[12], (N_ATTN, HEAD_DIM), 0.1),
        "lambda_q2": nrm(ks[13], (N_ATTN, HEAD_DIM), 0.1),
        "lambda_k2": nrm(ks[14], (N_ATTN, HEAD_DIM), 0.1),
        "subln_gain": 1.0 + nrm(ks[15], (N_ATTN, V_DIM), 0.05),
        "pool_w": nrm(ks[16], (N_POOL, N_POOL_GROUPS, POOL_GROUP, POOL_GROUP), POOL_GROUP ** -0.5),
        "pool_scale": 0.5 + nrm(ks[17], (N_POOL, D_MODEL), 0.05),
        "peer_wq": nrm(ks[18], (DEPTH, D_MODEL, PEER_HEADS * D_KEY), D_MODEL ** -0.5),
        "peer_keys": nrm(ks[19], (DEPTH, PEER_HEADS, 2, N_KEYS, D_HALF), D_HALF ** -0.5),
        "peer_u": nrm(ks[20], (DEPTH, N_EXPERTS, D_MODEL), D_MODEL ** -0.5),
        "peer_v": nrm(ks[21], (DEPTH, N_EXPERTS, D_MODEL), PEER_HEADS ** -0.5),
    }


def reference(x_prompt, x_sample, cache_k, cache_v, state_pool, meta_tokens, norm_mix, norm_ffn,
              norm_final, w_qkv, w_o, lambda_q1, lambda_k1, lambda_q2, lambda_k2, subln_gain,
              pool_w, pool_scale, peer_wq, peer_keys, peer_u, peer_v):
    B = x_prompt.shape[0]
    meta = jnp.broadcast_to(meta_tokens[None].astype(x_prompt.dtype), (B, N_META, D_MODEL))
    hp = jnp.concatenate([meta, x_prompt], axis=1)
    hs = x_sample
    kp_l, vp_l, ks_l, vs_l, pp_l, ps_l = [], [], [], [], [], []
    for i in range(DEPTH):
        j = i // N_MIXERS
        up = _rmsnorm(hp, norm_mix[i])
        us = _rmsnorm(hs, norm_mix[i])
        if i % N_MIXERS == 0:
            lam_init = 0.8 - 0.6 * math.exp(-0.3 * i)
            lam = _diff_lambda(lambda_q1[j], lambda_k1[j], lambda_q2[j], lambda_k2[j], lam_init)
            yp, kp, vp = _diff_attn_prompt(up, w_qkv[j], w_o[j], lam, lam_init, subln_gain[j])
            ys, ksn, vsn = _diff_attn_sample(us, cache_k[j], cache_v[j], w_qkv[j], w_o[j],
                                             lam, lam_init, subln_gain[j])
            kp_l.append(kp)
            vp_l.append(vp)
            ks_l.append(ksn)
            vs_l.append(vsn)
        else:
            xpp = jnp.concatenate([jnp.zeros((B, POOL_STATE, D_MODEL), up.dtype), up], axis=1)
            xps = jnp.concatenate([state_pool[j].astype(us.dtype), us], axis=1)
            yp = _multi_pool(xpp, 0, pool_w[j], pool_scale[j])
            ys = _multi_pool(xps, PAST_LEN, pool_w[j], pool_scale[j])
            pp_l.append(xpp[:, -POOL_STATE:])
            ps_l.append(xps[:, -POOL_STATE:])
        hp = hp + yp
        hs = hs + ys
        if i == DEPTH - 1:
            hp = hp[:, N_META:]
        hp = hp + _peer(_rmsnorm(hp, norm_ffn[i]), peer_wq[i], peer_keys[i], peer_u[i], peer_v[i])
        hs = hs + _peer(_rmsnorm(hs, norm_ffn[i]), peer_wq[i], peer_keys[i], peer_u[i], peer_v[i])
    y_prompt = _rmsnorm(hp, norm_final)
    y_sample = _rmsnorm(hs, norm_final)
    return (y_prompt, y_sample, jnp.stack(kp_l), jnp.stack(vp_l), jnp.stack(pp_l),
            jnp.stack(ks_l), jnp.stack(vs_l), jnp.stack(ps_l))
```

```python
import functools
import math

import numpy as np
import jax
import jax.numpy as jnp
from jax import lax
from jax.experimental import pallas as pl
from jax.experimental.pallas import tpu as pltpu

D_MODEL = 1024
SEQ = 8192
CHUNK = 64
N_META = 16
N_HEADS = 8
HEAD_DIM = 64
V_DIM = 2 * HEAD_DIM
QK_WIDTH = N_HEADS * 2 * HEAD_DIM
ROPE_THETA = 10000.0
POOL_WINDOWS = (2, 4, 8, 16)
POOL_GROUP = D_MODEL // len(POOL_WINDOWS)
POOL_STATE = max(POOL_WINDOWS) - 1
PEER_HEADS = 8
N_KEYS = 128
D_HALF = 128
PEER_TOPK = 16
EPS = 1e-6
NEG = -1e30

LANES = 128
HALO = 16
ROW_TILE = 512
ATT_Q = 256
META_KEYS = 128
PEER_TOKENS = 512
PEER_CHUNK = 1024
LP = (SEQ + N_META + ATT_Q - 1) // ATT_Q * ATT_Q
VMEM_LIMIT = 56 * 1024 * 1024

F32 = jnp.float32
BF16 = jnp.bfloat16
NT_DIMS = (((1,), (1,)), ((), ()))


def _rms(x, g):
    return x * lax.rsqrt(jnp.mean(x * x, axis=-1, keepdims=True) + EPS) * g


def _params(*sem):
    return pltpu.CompilerParams(dimension_semantics=sem, vmem_limit_bytes=VMEM_LIMIT)


def _rope(a, cos, sin):
    lane = lax.broadcasted_iota(jnp.int32, (a.shape[0], LANES), 1)
    first = (lane % HEAD_DIM) < (HEAD_DIM // 2)
    outs = []
    for g in range(a.shape[1] // LANES):
        xg = a[:, g * LANES:(g + 1) * LANES]
        partner = jnp.where(first, pltpu.roll(xg, LANES - HEAD_DIM // 2, 1),
                            pltpu.roll(xg, HEAD_DIM // 2, 1))
        outs.append(xg * cos + partner * sin)
    return jnp.concatenate(outs, axis=1)


def _qkv_kernel(x_ref, g_ref, w_ref, cos_ref, sin_ref,
                q_ref, k_ref, v_ref, kb_ref, vb_ref, xn_scr):
    j = pl.program_id(1)

    @pl.when(j == 0)
    def _():
        xn_scr[...] = _rms(x_ref[...], g_ref[...]).astype(BF16)

    acc = jnp.dot(xn_scr[...], w_ref[...], preferred_element_type=F32)

    @pl.when(j == 0)
    def _():
        q_ref[...] = (_rope(acc, cos_ref[...], sin_ref[...]) * (HEAD_DIM ** -0.5)).astype(BF16)

    @pl.when(j == 1)
    def _():
        r = _rope(acc, cos_ref[...], sin_ref[...])
        k_ref[...] = r
        kb_ref[...] = r.astype(BF16)

    @pl.when(j == 2)
    def _():
        v_ref[...] = acc
        vb_ref[...] = acc.astype(BF16)


def _qkv(x, g, w, cos, sin, tm):
    n, d = x.shape
    tab_blocks = cos.shape[0] // tm
    row = lambda i, j: (i, 0)
    tab = lambda i, j: (i % tab_blocks, 0)
    return pl.pallas_call(
        _qkv_kernel,
        grid=(n // tm, 3),
        in_specs=[
            pl.BlockSpec((tm, d), row),
            pl.BlockSpec((1, d), lambda i, j: (0, 0)),
            pl.BlockSpec((d, QK_WIDTH), lambda i, j: (0, j)),
            pl.BlockSpec((tm, LANES), tab),
            pl.BlockSpec((tm, LANES), tab),
        ],
        out_specs=[pl.BlockSpec((tm, QK_WIDTH), row)] * 5,
        out_shape=[
            jax.ShapeDtypeStruct((n, QK_WIDTH), BF16),
            jax.ShapeDtypeStruct((n, QK_WIDTH), F32),
            jax.ShapeDtypeStruct((n, QK_WIDTH), F32),
            jax.ShapeDtypeStruct((n, QK_WIDTH), BF16),
            jax.ShapeDtypeStruct((n, QK_WIDTH), BF16),
        ],
        scratch_shapes=[pltpu.VMEM((tm, d), BF16)],
        compiler_params=_params("parallel", "arbitrary"),
        name="qkv",
    )(x, g, w, cos, sin)


def _stack_maps(q):
    lane = lax.broadcasted_iota(jnp.int32, q.shape, 1)
    qf = q.astype(F32)
    return jnp.concatenate([jnp.where(lane < HEAD_DIM, qf, 0.0),
                            jnp.where(lane >= HEAD_DIM, qf, 0.0)], axis=0).astype(BF16)


def _lambda(lq1, lk1, lq2, lk2, lam_init):
    return (jnp.exp(jnp.sum(lq1 * lk1, axis=-1, keepdims=True))
            - jnp.exp(jnp.sum(lq2 * lk2, axis=-1, keepdims=True)) + lam_init)


def _diff_finish(acc, l, lam, gain, lam_init):
    t = acc.shape[0] // 2
    o = acc / l
    o = o[:t] - lam * o[t:]
    return (_rms(o, gain) * (1.0 - lam_init)).astype(BF16)


def _attn_prompt_kernel(q_ref, k_ref, v_ref, gain_ref, lq1, lk1, lq2, lk2, o_ref,
                        qq_scr, m_scr, l_scr, acc_scr, *, n_real, lam_init):
    i = pl.program_id(2)
    tq = q_ref.shape[0]
    qq_scr[...] = _stack_maps(q_ref[...])
    m_scr[...] = jnp.full(m_scr.shape, NEG, F32)
    l_scr[...] = jnp.zeros(l_scr.shape, F32)
    acc_scr[...] = jnp.zeros(acc_scr.shape, F32)

    def attend(kblk, vblk, mask):
        s = lax.dot_general(qq_scr[...], kblk, NT_DIMS, preferred_element_type=F32)
        if mask is not None:
            s = jnp.where(mask, s, NEG)
        reps = s.shape[1] // LANES
        m_prev = m_scr[...]
        m_new = jnp.maximum(m_prev, jnp.max(s, axis=1, keepdims=True))
        alpha = jnp.exp(m_prev - m_new)
        p = jnp.exp(s - (pltpu.repeat(m_new, reps, axis=1) if reps > 1 else m_new))
        l_scr[...] = alpha * l_scr[...] + jnp.sum(p, axis=1, keepdims=True)
        acc_scr[...] = alpha * acc_scr[...] + jnp.dot(p.astype(BF16), vblk,
                                                      preferred_element_type=F32)
        m_scr[...] = m_new

    meta0 = n_real * tq
    col_m = lax.broadcasted_iota(jnp.int32, (2 * tq, META_KEYS), 1)
    attend(k_ref[pl.ds(meta0, META_KEYS), :], v_ref[pl.ds(meta0, META_KEYS), :], col_m < N_META)

    @pl.when(i < n_real)
    def _():
        def body(j, carry):
            off = pl.multiple_of(j * tq, tq)
            attend(k_ref[pl.ds(off, tq), :], v_ref[pl.ds(off, tq), :], None)
            return carry

        lax.fori_loop(0, i, body, 0)
        off = pl.multiple_of(i * tq, tq)
        r = lax.broadcasted_iota(jnp.int32, (2 * tq, tq), 0)
        c = lax.broadcasted_iota(jnp.int32, (2 * tq, tq), 1)
        mask = (c // CHUNK) <= ((r % tq) // CHUNK)
        attend(k_ref[pl.ds(off, tq), :], v_ref[pl.ds(off, tq), :], mask)

    lam = _lambda(lq1[...], lk1[...], lq2[...], lk2[...], lam_init)
    o_ref[...] = _diff_finish(acc_scr[...], l_scr[...], lam, gain_ref[...], lam_init)


def _attn_prompt(q, kb, vb, gain, lams, lam_init):
    b, lp, _ = q.shape
    tq = ATT_Q
    nq = lp // tq
    qspec = pl.BlockSpec((None, tq, LANES), lambda bi, h, i: (bi, i, h))
    kvspec = pl.BlockSpec((None, lp, LANES), lambda bi, h, i: (bi, 0, h))
    small = lambda n: pl.BlockSpec((1, n), lambda bi, h, i: (0, 0))
    return pl.pallas_call(
        functools.partial(_attn_prompt_kernel, n_real=SEQ // tq, lam_init=lam_init),
        grid=(b, N_HEADS, nq),
        in_specs=[qspec, kvspec, kvspec, small(V_DIM)] + [small(HEAD_DIM)] * 4,
        out_specs=qspec,
        out_shape=jax.ShapeDtypeStruct((b, lp, N_HEADS * V_DIM), BF16),
        scratch_shapes=[
            pltpu.VMEM((2 * tq, LANES), BF16),
            pltpu.VMEM((2 * tq, LANES), F32),
            pltpu.VMEM((2 * tq, LANES), F32),
            pltpu.VMEM((2 * tq, V_DIM), F32),
        ],
        compiler_params=_params("parallel", "parallel", "arbitrary"),
        name="attn_prompt",
    )(q, kb, vb, gain, *lams)


def _attn_sample_kernel(q_ref, kn_ref, vn_ref, ck_ref, cv_ref, gain_ref, lq1, lk1, lq2, lk2,
                        o_ref, *, lam_init):
    qq = _stack_maps(q_ref[...])
    ck = ck_ref[...].astype(BF16)
    cv = cv_ref[...].astype(BF16)
    s_c = lax.dot_general(qq, ck, NT_DIMS, preferred_element_type=F32)
    s_n = lax.dot_general(qq, kn_ref[...], NT_DIMS, preferred_element_type=F32)
    m = jnp.maximum(jnp.max(s_c, axis=1, keepdims=True), jnp.max(s_n, axis=1, keepdims=True))
    p_c = jnp.exp(s_c - m)
    p_n = jnp.exp(s_n - m)
    l = jnp.sum(p_c, axis=1, keepdims=True) + jnp.sum(p_n, axis=1, keepdims=True)
    acc = (jnp.dot(p_c.astype(BF16), cv, preferred_element_type=F32)
           + jnp.dot(p_n.astype(BF16), vn_ref[...], preferred_element_type=F32))
    lam = _lambda(lq1[...], lk1[...], lq2[...], lk2[...], lam_init)
    o_ref[...] = _diff_finish(acc, l, lam, gain_ref[...], lam_init)


def _attn_sample(q, kb, vb, ck, cv, gain, lams, lam_init):
    b, s, _ = q.shape
    past = ck.shape[1]
    nspec = pl.BlockSpec((None, s, LANES), lambda bi, h: (bi, 0, h))
    cspec = pl.BlockSpec((None, past, LANES), lambda bi, h: (bi, 0, h))
    small = lambda n: pl.BlockSpec((1, n), lambda bi, h: (0, 0))
    return pl.pallas_call(
        functools.partial(_attn_sample_kernel, lam_init=lam_init),
        grid=(b, N_HEADS),
        in_specs=[nspec, nspec, nspec, cspec, cspec, small(V_DIM)] + [small(HEAD_DIM)] * 4,
        out_specs=nspec,
        out_shape=jax.ShapeDtypeStruct((b, s, N_HEADS * V_DIM), BF16),
        compiler_params=_params("parallel", "parallel"),
        name="attn_sample",
    )(q, kb, vb, ck, cv, gain, *lams)


def _oproj_kernel(o_ref, w_ref, h_ref, out_ref):
    out_ref[...] = h_ref[...] + jnp.dot(o_ref[...], w_ref[...], preferred_element_type=F32)


def _oproj(o, w, h, tm):
    n, d = h.shape
    row = lambda i: (i, 0)
    return pl.pallas_call(
        _oproj_kernel,
        grid=(n // tm,),
        in_specs=[pl.BlockSpec((tm, o.shape[1]), row),
                  pl.BlockSpec(w.shape, lambda i: (0, 0)),
                  pl.BlockSpec((tm, d), row)],
        out_specs=pl.BlockSpec((tm, d), row),
        out_shape=jax.ShapeDtypeStruct((n, d), F32),
        compiler_params=_params("parallel"),
        name="oproj",
    )(o, w, h)


def _pool_kernel(h_ref, halo_ref, g_ref, w_ref, scale_ref, out_ref, state_ref, xe_scr,
                 *, halo_normed):
    tm = h_ref.shape[0]
    x = h_ref[...]
    g = g_ref[...]
    u = _rms(x, g)
    halo = halo_ref[...]
    xe_scr[0:HALO, :] = halo if halo_normed else _rms(halo, g)
    xe_scr[HALO:, :] = u
    ys = []
    for gi, w in enumerate(POOL_WINDOWS):
        sl = slice(gi * POOL_GROUP, (gi + 1) * POOL_GROUP)
        wsum = u[:, sl]
        for dlt in range(1, w):
            wsum = wsum + xe_scr[pl.ds(HALO - dlt, tm), sl]
        dgrp = wsum / float(w) - u[:, sl]
        ys.append(jnp.dot(dgrp.astype(BF16), w_ref[gi], preferred_element_type=F32))
    y = jnp.concatenate(ys, axis=1) * scale_ref[...]
    out_ref[...] = x + y
    state_ref[...] = u[tm - HALO:, :]


def _pool(h, halo_src, halo_map, g, w, scale, tm, nblk, halo_normed):
    b, _, d = h.shape
    row = lambda bi, i: (bi, i, 0)
    const2 = lambda bi, i: (0, 0)
    return pl.pallas_call(
        functools.partial(_pool_kernel, halo_normed=halo_normed),
        grid=(b, nblk),
        in_specs=[pl.BlockSpec((None, tm, d), row),
                  pl.BlockSpec((None, HALO, d), halo_map),
                  pl.BlockSpec((1, d), const2),
                  pl.BlockSpec(w.shape, lambda bi, i: (0, 0, 0)),
                  pl.BlockSpec((1, d), const2)],
        out_specs=[pl.BlockSpec((None, tm, d), row),
                   pl.BlockSpec((None, HALO, d), lambda bi, i: (bi, 0, 0))],
        out_shape=[jax.ShapeDtypeStruct((b, nblk * tm, d), F32),
                   jax.ShapeDtypeStruct((b, HALO, d), F32)],
        scratch_shapes=[pltpu.VMEM((HALO + tm, d), F32)],
        compiler_params=_params("parallel", "arbitrary"),
        name="pool",
    )(h, halo_src, g, w, scale)


def _top_values(s, n):
    vals = []
    for _ in range(n):
        m = jnp.max(s, axis=0, keepdims=True)
        vals.append(m)
        s = jnp.where(s >= m, NEG, s)
    return vals


def _route(s1, s2):
    a = _top_values(s1, PEER_TOPK)
    b = _top_values(s2, PEER_TOPK)
    bmat = jnp.concatenate(b, axis=0)
    comb = jnp.concatenate([ai + bmat for ai in a], axis=0)
    tops = _top_values(comb, PEER_TOPK + 1)
    thr = 0.5 * (tops[PEER_TOPK - 1] + tops[PEER_TOPK])
    cmax = a[0] + b[0]
    z = jnp.sum(jnp.where(comb >= thr, jnp.exp(comb - cmax), 0.0), axis=0, keepdims=True)
    return thr - s1, jnp.exp(s1 - a[0]) / z, jnp.exp(s2 - b[0])


def _peer_kernel(h_ref, g_ref, wq_ref, keys_ref, u_ref, vt_ref, gf_ref, out_ref,
                 tb_scr, sc_scr, e_scr, hid_scr, coef_scr, acc_scr, *, final_norm):
    c = pl.program_id(1)
    tb_n = h_ref.shape[0]
    nlb = tb_n // LANES
    rows_per_chunk = u_ref.shape[0] // N_KEYS

    @pl.when(c == 0)
    def _():
        tb = _rms(h_ref[...], g_ref[...]).astype(BF16)
        tb_scr[...] = tb
        q = jnp.dot(tb, wq_ref[...], preferred_element_type=F32).astype(BF16)
        for hp in range(2 * PEER_HEADS):
            s = lax.dot_general(keys_ref[hp], q[:, hp * D_HALF:(hp + 1) * D_HALF], NT_DIMS,
                                preferred_element_type=F32)
            for lb in range(nlb):
                sc_scr[hp, lb] = s[:, lb * LANES:(lb + 1) * LANES]

        def route(idx, carry):
            h = idx // nlb
            lb = idx % nlb
            r1, e1, e2 = _route(sc_scr[2 * h, lb], sc_scr[2 * h + 1, lb])
            sc_scr[2 * h, lb] = r1
            e_scr[2 * h, lb] = e1
            e_scr[2 * h + 1, lb] = e2
            return carry

        lax.fori_loop(0, PEER_HEADS * nlb, route, 0)
        acc_scr[...] = jnp.zeros(acc_scr.shape, F32)

    hid = lax.dot_general(u_ref[...], tb_scr[...], NT_DIMS, preferred_element_type=F32)
    for lb in range(nlb):
        hid_scr[lb] = hid[:, lb * LANES:(lb + 1) * LANES]

    def key_row(i1, carry):
        row = c * rows_per_chunk + i1
        off = pl.multiple_of(i1 * N_KEYS, N_KEYS)
        for lb in range(nlb):
            gate = jnp.zeros((N_KEYS, LANES), F32)
            for h in range(PEER_HEADS):
                r1 = sc_scr[2 * h, lb, pl.ds(row, 1), :]
                e1 = e_scr[2 * h, lb, pl.ds(row, 1), :]
                gate = gate + jnp.where(sc_scr[2 * h + 1, lb] >= r1, e_scr[2 * h + 1, lb] * e1, 0.0)
            hd = hid_scr[lb, pl.ds(off, N_KEYS), :]
            act = 0.5 * hd * (1.0 + lax.erf(hd * np.float32(math.sqrt(0.5))))
            coef_scr[pl.ds(off, N_KEYS), lb * LANES:(lb + 1) * LANES] = (gate * act).astype(BF16)
        return carry

    lax.fori_loop(0, rows_per_chunk, key_row, 0)
    acc_scr[...] += jnp.dot(vt_ref[...], coef_scr[...], preferred_element_type=F32)

    @pl.when(c == pl.num_programs(1) - 1)
    def _():
        y = h_ref[...] + acc_scr[...].T
        out_ref[...] = _rms(y, gf_ref[...]) if final_norm else y


def _peer(h, g, wq, keys, u, vt, gf, final_norm, tb=PEER_TOKENS, ec=PEER_CHUNK):
    n, d = h.shape
    n_exp = u.shape[0]
    nlb = tb // LANES
    row = lambda i, c: (i, 0)
    const2 = lambda i, c: (0, 0)
    return pl.pallas_call(
        functools.partial(_peer_kernel, final_norm=final_norm),
        grid=(n // tb, n_exp // ec),
        in_specs=[pl.BlockSpec((tb, d), row),
                  pl.BlockSpec((1, d), const2),
                  pl.BlockSpec(wq.shape, const2),
                  pl.BlockSpec(keys.shape, lambda i, c: (0, 0, 0)),
                  pl.BlockSpec((ec, d), lambda i, c: (c, 0)),
                  pl.BlockSpec((d, ec), lambda i, c: (0, c)),
                  pl.BlockSpec((1, d), const2)],
        out_specs=pl.BlockSpec((tb, d), row),
        out_shape=jax.ShapeDtypeStruct((n, d), F32),
        scratch_shapes=[
            pltpu.VMEM((tb, d), BF16),
            pltpu.VMEM((2 * PEER_HEADS, nlb, N_KEYS, LANES), F32),
            pltpu.VMEM((2 * PEER_HEADS, nlb, N_KEYS, LANES), F32),
            pltpu.VMEM((nlb, ec, LANES), F32),
            pltpu.VMEM((ec, tb), BF16),
            pltpu.VMEM((d, tb), F32),
        ],
        compiler_params=_params("parallel", "arbitrary"),
        name="peer",
    )(h, g, wq, keys, u, vt, gf)


def _rope_tables(pos):
    half = HEAD_DIM // 2
    inv = ROPE_THETA ** (-jnp.arange(half, dtype=F32) / half)
    ang = pos.astype(F32)[:, None] * inv[None, :]
    sign = jnp.tile(jnp.concatenate([-jnp.ones((half,), F32), jnp.ones((half,), F32)]),
                    LANES // HEAD_DIM)
    reps = LANES // half
    return jnp.tile(jnp.cos(ang), (1, reps)), jnp.tile(jnp.sin(ang), (1, reps)) * sign[None, :]


def kernel(x_prompt, x_sample, cache_k, cache_v, state_pool, meta_tokens, norm_mix, norm_ffn,
           norm_final, w_qkv, w_o, lambda_q1, lambda_k1, lambda_q2, lambda_k2, subln_gain,
           pool_w, pool_scale, peer_wq, peer_keys, peer_u, peer_v):
    b, seq, d = x_prompt.shape
    sb, ss, _ = x_sample.shape
    past = cache_k.shape[2]
    row2 = lambda a: a.reshape(1, -1)

    meta = jnp.broadcast_to(meta_tokens[None].astype(x_prompt.dtype), (b, N_META, d))
    hp = jnp.concatenate([x_prompt, meta, jnp.zeros((b, LP - seq - N_META, d), F32)], axis=1)
    hp = hp.reshape(b * LP, d)
    hs = x_sample.reshape(sb * ss, d)

    pos_p = jnp.concatenate([N_META + jnp.arange(seq), jnp.arange(N_META),
                             jnp.zeros((LP - seq - N_META,), jnp.int32)])
    tab_reps = math.lcm(LP, ROW_TILE) // LP
    cos_p, sin_p = (jnp.tile(t, (tab_reps, 1)) for t in _rope_tables(pos_p))
    pos_s = jnp.tile(past + jnp.arange(ss), ROW_TILE // ss)
    cos_s, sin_s = _rope_tables(pos_s)

    peer_w = []
    for i in range(2):
        peer_w.append((row2(norm_ffn[i]), peer_wq[i].astype(BF16),
                       peer_keys[i].reshape(2 * PEER_HEADS, N_KEYS, D_HALF).astype(BF16),
                       peer_u[i].astype(BF16), peer_v[i].T.astype(BF16)))
    gfin = row2(norm_final)

    lam_init = 0.8 - 0.6 * math.exp(-0.3 * 0)
    wqkv = w_qkv[0].astype(BF16)
    wo = w_o[0].astype(BF16)
    gain = row2(subln_gain[0])
    lams = (row2(lambda_q1[0]), row2(lambda_k1[0]), row2(lambda_q2[0]), row2(lambda_k2[0]))
    g0 = row2(norm_mix[0])

    q, k, v, kb, vb = _qkv(hp, g0, wqkv, cos_p, sin_p, ROW_TILE)
    as3 = lambda a: a.reshape(b, LP, -1)
    o = _attn_prompt(as3(q), as3(kb), as3(vb), gain, lams, lam_init)
    hp = _oproj(o.reshape(b * LP, -1), wo, hp, ROW_TILE)

    qs, ks, vs, kbs, vbs = _qkv(hs, g0, wqkv, cos_s, sin_s, ROW_TILE)
    ss3 = lambda a: a.reshape(sb, ss, -1)
    o_s = _attn_sample(ss3(qs), ss3(kbs), ss3(vbs), cache_k[0].reshape(sb, past, -1),
                       cache_v[0].reshape(sb, past, -1), gain, lams, lam_init)
    hs = _oproj(o_s.reshape(sb * ss, -1), wo, hs, ROW_TILE)

    hp = _peer(hp, *peer_w[0], gfin, False)
    hs = _peer(hs, *peer_w[0], gfin, False)

    g1 = row2(norm_mix[1])
    pw = pool_w[0].astype(BF16)
    psc = row2(pool_scale[0])
    hp3 = hp.reshape(b, LP, d)
    blocks_per_tile = ROW_TILE // HALO
    meta_blk = seq // HALO
    halo_p = lambda bi, i: (bi, jnp.where(i == 0, meta_blk, i * blocks_per_tile - 1), 0)
    hp3, pool_p = _pool(hp3, hp3, halo_p, g1, pw, psc, ROW_TILE, seq // ROW_TILE, False)
    hs3 = hs.reshape(sb, ss, d)
    state = jnp.pad(state_pool[0], ((0, 0), (HALO - POOL_STATE, 0), (0, 0)))
    hs3, pool_s = _pool(hs3, state, lambda bi, i: (bi, 0, 0), g1, pw, psc, ss, 1, True)

    y_p = _peer(hp3.reshape(b * seq, d), *peer_w[1], gfin, True).reshape(b, seq, d)
    y_s = _peer(hs3.reshape(sb * ss, d), *peer_w[1], gfin, True).reshape(sb, ss, d)

    def prompt_cache(a):
        a = a.reshape(b, LP, N_HEADS, -1)
        return jnp.concatenate([a[:, seq:seq + N_META], a[:, :seq]], axis=1)[None]

    return (y_p, y_s, prompt_cache(k), prompt_cache(v), pool_p[:, HALO - POOL_STATE:][None],
            ks.reshape(1, sb, ss, N_HEADS, -1), vs.reshape(1, sb, ss, N_HEADS, -1),
            pool_s[:, HALO - POOL_STATE:][None])
```

```python
import functools
import math

import numpy as np
import jax
import jax.numpy as jnp
from jax import lax
from jax.experimental import pallas as pl
from jax.experimental.pallas import tpu as pltpu

D_MODEL = 1024
SEQ = 8192
CHUNK = 64
N_META = 16
N_HEADS = 8
HEAD_DIM = 64
V_DIM = 2 * HEAD_DIM
QK_WIDTH = N_HEADS * 2 * HEAD_DIM
ROPE_THETA = 10000.0
POOL_WINDOWS = (2, 4, 8, 16)
POOL_GROUP = D_MODEL // len(POOL_WINDOWS)
POOL_STATE = max(POOL_WINDOWS) - 1
PEER_HEADS = 8
N_KEYS = 128
D_HALF = 128
PEER_TOPK = 16
EPS = 1e-6
NEG = -1e30

LANES = 128
SUBLANES = 8
HALO = 16
ROW_TILE = 512
ATT_Q = 256
ATT_GROUP = 4
META_KEYS = 128
PEER_TOKENS = 512
PEER_CHUNK = 1024
MXU_N = 256
MXU_PARTS = 4
LP = (SEQ + N_META + ATT_Q - 1) // ATT_Q * ATT_Q
VMEM_LIMIT = 58 * 1024 * 1024

F32 = jnp.float32
BF16 = jnp.bfloat16
U32 = jnp.uint32
NT_DIMS = (((1,), (1,)), ((), ()))
LOG2E = math.log2(math.e)


def _rms(x, g):
    return x * lax.rsqrt(jnp.mean(x * x, axis=-1, keepdims=True) + EPS) * g


def _params(*sem):
    return pltpu.CompilerParams(dimension_semantics=sem, vmem_limit_bytes=VMEM_LIMIT)


def _resident(shape, index_map):
    return pl.BlockSpec(shape, index_map, pipeline_mode=pl.Buffered(1))


def _rope(a, cos, sin):
    lane = lax.broadcasted_iota(jnp.int32, (a.shape[0], LANES), 1)
    first = (lane % HEAD_DIM) < (HEAD_DIM // 2)
    outs = []
    for g in range(a.shape[1] // LANES):
        xg = a[:, g * LANES:(g + 1) * LANES]
        partner = jnp.where(first, pltpu.roll(xg, LANES - HEAD_DIM // 2, 1),
                            pltpu.roll(xg, HEAD_DIM // 2, 1))
        outs.append(xg * cos + partner * sin)
    return jnp.concatenate(outs, axis=1)


def _qkv_kernel(x_ref, g_ref, w_ref, cos_ref, sin_ref,
                q_ref, k_ref, v_ref, kb_ref, vb_ref, xn_scr):
    j = pl.program_id(1)

    @pl.when(j == 0)
    def _():
        xn_scr[...] = _rms(x_ref[...], g_ref[...]).astype(BF16)

    acc = jnp.dot(xn_scr[...], w_ref[...], preferred_element_type=F32)

    @pl.when(j == 0)
    def _():
        q_ref[...] = (_rope(acc, cos_ref[...], sin_ref[...])
                      * (HEAD_DIM ** -0.5 * LOG2E)).astype(BF16)

    @pl.when(j == 1)
    def _():
        r = _rope(acc, cos_ref[...], sin_ref[...])
        k_ref[...] = r
        kb_ref[...] = r.astype(BF16)

    @pl.when(j == 2)
    def _():
        v_ref[...] = acc
        vb_ref[...] = acc.astype(BF16)


def _qkv(x, g, w, cos, sin, tm):
    n, d = x.shape
    tab_blocks = cos.shape[0] // tm
    row = lambda i, j: (i, 0)
    tab = lambda i, j: (i % tab_blocks, 0)
    return pl.pallas_call(
        _qkv_kernel,
        grid=(n // tm, 3),
        in_specs=[
            pl.BlockSpec((tm, d), row),
            pl.BlockSpec((1, d), lambda i, j: (0, 0)),
            pl.BlockSpec((d, QK_WIDTH), lambda i, j: (0, j)),
            pl.BlockSpec((tm, LANES), tab),
            pl.BlockSpec((tm, LANES), tab),
        ],
        out_specs=[pl.BlockSpec((tm, QK_WIDTH), row)] * 5,
        out_shape=[
            jax.ShapeDtypeStruct((n, QK_WIDTH), BF16),
            jax.ShapeDtypeStruct((n, QK_WIDTH), F32),
            jax.ShapeDtypeStruct((n, QK_WIDTH), F32),
            jax.ShapeDtypeStruct((n, QK_WIDTH), BF16),
            jax.ShapeDtypeStruct((n, QK_WIDTH), BF16),
        ],
        scratch_shapes=[pltpu.VMEM((tm, d), BF16)],
        compiler_params=_params("parallel", "arbitrary"),
        name="qkv",
    )(x, g, w, cos, sin)


def _stack_maps(q):
    lane = lax.broadcasted_iota(jnp.int32, q.shape, 1)
    qf = q.astype(F32)
    return jnp.concatenate([jnp.where(lane < HEAD_DIM, qf, 0.0),
                            jnp.where(lane >= HEAD_DIM, qf, 0.0)], axis=0).astype(BF16)


def _lambda(lq1, lk1, lq2, lk2, lam_init):
    return (jnp.exp(jnp.sum(lq1 * lk1, axis=-1, keepdims=True))
            - jnp.exp(jnp.sum(lq2 * lk2, axis=-1, keepdims=True)) + lam_init)


def _diff_finish(acc, l, lam, gain, lam_init):
    t = acc.shape[0] // 2
    o = acc / l
    o = o[:t] - lam * o[t:]
    return (_rms(o, gain) * (1.0 - lam_init)).astype(BF16)


def _attn_prompt_kernel(q_ref, k_ref, v_ref, gain_ref, lq1, lk1, lq2, lk2, o_ref,
                        qq_scr, m_scr, l_scr, acc_scr, *, n_real, lam_init):
    i = pl.program_id(2)
    tq = q_ref.shape[0]
    qq_scr[...] = _stack_maps(q_ref[...])
    m_scr[...] = jnp.full(m_scr.shape, NEG, F32)
    l_scr[...] = jnp.zeros(l_scr.shape, F32)
    acc_scr[...] = jnp.zeros(acc_scr.shape, F32)

    def attend(start, width, mask):
        kblk = k_ref[pl.ds(start, width), :]
        vblk = v_ref[pl.ds(start, width), :]
        s = lax.dot_general(qq_scr[...], kblk, NT_DIMS, preferred_element_type=F32)
        if mask is not None:
            s = jnp.where(mask, s, NEG)
        reps = width // LANES
        m_prev = m_scr[...]
        m_new = jnp.maximum(m_prev, jnp.max(s, axis=1, keepdims=True))
        alpha = jnp.exp2(m_prev - m_new)
        p = jnp.exp2(s - (pltpu.repeat(m_new, reps, axis=1) if reps > 1 else m_new))
        l_scr[...] = alpha * l_scr[...] + jnp.sum(p, axis=1, keepdims=True)
        acc_scr[...] = alpha * acc_scr[...] + jnp.dot(p.astype(BF16), vblk,
                                                      preferred_element_type=F32)
        m_scr[...] = m_new

    col_m = lax.broadcasted_iota(jnp.int32, (2 * tq, META_KEYS), 1)
    attend(n_real * tq, META_KEYS, col_m < N_META)

    @pl.when(i < n_real)
    def _():
        group = ATT_GROUP * tq
        n_groups = i // ATT_GROUP

        def body(j, carry):
            attend(pl.multiple_of(j * group, group), group, None)
            return carry

        lax.fori_loop(0, n_groups, body, 0)
        rem = i - n_groups * ATT_GROUP
        width = ATT_GROUP // 2
        while width >= 1:
            start_blk = n_groups * ATT_GROUP + (rem // (2 * width)) * (2 * width)

            @pl.when((rem // width) % 2 == 1)
            def _(start_blk=start_blk, width=width):
                attend(pl.multiple_of(start_blk * tq, tq), width * tq, None)

            width //= 2
        r = lax.broadcasted_iota(jnp.int32, (2 * tq, tq), 0)
        c = lax.broadcasted_iota(jnp.int32, (2 * tq, tq), 1)
        attend(pl.multiple_of(i * tq, tq), tq, (c // CHUNK) <= ((r % tq) // CHUNK))

    lam = _lambda(lq1[...], lk1[...], lq2[...], lk2[...], lam_init)
    o_ref[...] = _diff_finish(acc_scr[...], l_scr[...], lam, gain_ref[...], lam_init)


def _attn_prompt(q, kb, vb, gain, lams, lam_init, n_real):
    b, lp, _ = q.shape
    tq = ATT_Q
    nq = lp // tq
    qspec = pl.BlockSpec((None, tq, LANES), lambda bi, h, i: (bi, i, h))
    kvspec = pl.BlockSpec((None, lp, LANES), lambda bi, h, i: (bi, 0, h))
    small = lambda n: pl.BlockSpec((1, n), lambda bi, h, i: (0, 0))
    return pl.pallas_call(
        functools.partial(_attn_prompt_kernel, n_real=n_real, lam_init=lam_init),
        grid=(b, N_HEADS, nq),
        in_specs=[qspec, kvspec, kvspec, small(V_DIM)] + [small(HEAD_DIM)] * 4,
        out_specs=qspec,
        out_shape=jax.ShapeDtypeStruct((b, lp, N_HEADS * V_DIM), BF16),
        scratch_shapes=[
            pltpu.VMEM((2 * tq, LANES), BF16),
            pltpu.VMEM((2 * tq, LANES), F32),
            pltpu.VMEM((2 * tq, LANES), F32),
            pltpu.VMEM((2 * tq, V_DIM), F32),
        ],
        compiler_params=_params("parallel", "parallel", "arbitrary"),
        name="attn_prompt",
    )(q, kb, vb, gain, *lams)


def _attn_sample_kernel(q_ref, kn_ref, vn_ref, ck_ref, cv_ref, gain_ref, lq1, lk1, lq2, lk2,
                        o_ref, *, lam_init):
    qq = _stack_maps(q_ref[...])
    ck = ck_ref[...].astype(BF16)
    cv = cv_ref[...].astype(BF16)
    s_c = lax.dot_general(qq, ck, NT_DIMS, preferred_element_type=F32)
    s_n = lax.dot_general(qq, kn_ref[...], NT_DIMS, preferred_element_type=F32)
    m = jnp.maximum(jnp.max(s_c, axis=1, keepdims=True), jnp.max(s_n, axis=1, keepdims=True))
    p_c = jnp.exp2(s_c - m)
    p_n = jnp.exp2(s_n - m)
    l = jnp.sum(p_c, axis=1, keepdims=True) + jnp.sum(p_n, axis=1, keepdims=True)
    acc = (jnp.dot(p_c.astype(BF16), cv, preferred_element_type=F32)
           + jnp.dot(p_n.astype(BF16), vn_ref[...], preferred_element_type=F32))
    lam = _lambda(lq1[...], lk1[...], lq2[...], lk2[...], lam_init)
    o_ref[...] = _diff_finish(acc, l, lam, gain_ref[...], lam_init)


def _attn_sample(q, kb, vb, ck, cv, gain, lams, lam_init):
    b, s, _ = q.shape
    past = ck.shape[1]
    nspec = pl.BlockSpec((None, s, LANES), lambda bi, h: (bi, 0, h))
    cspec = pl.BlockSpec((None, past, LANES), lambda bi, h: (bi, 0, h))
    small = lambda n: pl.BlockSpec((1, n), lambda bi, h: (0, 0))
    return pl.pallas_call(
        functools.partial(_attn_sample_kernel, lam_init=lam_init),
        grid=(b, N_HEADS),
        in_specs=[nspec, nspec, nspec, cspec, cspec, small(V_DIM)] + [small(HEAD_DIM)] * 4,
        out_specs=nspec,
        out_shape=jax.ShapeDtypeStruct((b, s, N_HEADS * V_DIM), BF16),
        compiler_params=_params("parallel", "parallel"),
        name="attn_sample",
    )(q, kb, vb, ck, cv, gain, *lams)


def _oproj_kernel(o_ref, w_ref, h_ref, out_ref):
    out_ref[...] = h_ref[...] + jnp.dot(o_ref[...], w_ref[...], preferred_element_type=F32)


def _oproj(o, w, h, tm):
    n, d = h.shape
    row = lambda i: (i, 0)
    return pl.pallas_call(
        _oproj_kernel,
        grid=(n // tm,),
        in_specs=[pl.BlockSpec((tm, o.shape[1]), row),
                  pl.BlockSpec(w.shape, lambda i: (0, 0)),
                  pl.BlockSpec((tm, d), row)],
        out_specs=pl.BlockSpec((tm, d), row),
        out_shape=jax.ShapeDtypeStruct((n, d), F32),
        compiler_params=_params("parallel"),
        name="oproj",
    )(o, w, h)


def _pool_kernel(h_ref, halo_ref, g_ref, w_ref, scale_ref, out_ref, state_ref, xe_scr,
                 *, halo_normed):
    tm = h_ref.shape[0]
    x = h_ref[...]
    g = g_ref[...]
    u = _rms(x, g)
    halo = halo_ref[...]
    xe_scr[0:HALO, :] = halo if halo_normed else _rms(halo, g)
    xe_scr[HALO:, :] = u
    ys = []
    for gi, w in enumerate(POOL_WINDOWS):
        sl = slice(gi * POOL_GROUP, (gi + 1) * POOL_GROUP)
        wsum = u[:, sl]
        for dlt in range(1, w):
            wsum = wsum + xe_scr[pl.ds(HALO - dlt, tm), sl]
        dgrp = wsum / float(w) - u[:, sl]
        ys.append(jnp.dot(dgrp.astype(BF16), w_ref[gi], preferred_element_type=F32))
    y = jnp.concatenate(ys, axis=1) * scale_ref[...]
    out_ref[...] = x + y
    state_ref[...] = u[tm - HALO:, :]


def _pool(h, halo_src, halo_map, g, w, scale, tm, nblk, halo_normed):
    b, _, d = h.shape
    row = lambda bi, i: (bi, i, 0)
    const2 = lambda bi, i: (0, 0)
    return pl.pallas_call(
        functools.partial(_pool_kernel, halo_normed=halo_normed),
        grid=(b, nblk),
        in_specs=[pl.BlockSpec((None, tm, d), row),
                  pl.BlockSpec((None, HALO, d), halo_map),
                  pl.BlockSpec((1, d), const2),
                  pl.BlockSpec(w.shape, lambda bi, i: (0, 0, 0)),
                  pl.BlockSpec((1, d), const2)],
        out_specs=[pl.BlockSpec((None, tm, d), row),
                   pl.BlockSpec((None, HALO, d), lambda bi, i: (bi, 0, 0))],
        out_shape=[jax.ShapeDtypeStruct((b, nblk * tm, d), F32),
                   jax.ShapeDtypeStruct((b, HALO, d), F32)],
        scratch_shapes=[pltpu.VMEM((HALO + tm, d), F32)],
        compiler_params=_params("parallel", "arbitrary"),
        name="pool",
    )(h, halo_src, g, w, scale)


def _ranked_top(s, n):
    vals = []
    pos = jnp.full(s.shape, float(n), F32)
    for k in range(n):
        m = jnp.max(s, axis=0, keepdims=True)
        vals.append(m)
        hit = s >= m
        pos = jnp.where(hit, float(k), pos)
        s = jnp.where(hit, NEG, s)
    return vals, pos


def _route(s1, s2):
    n = PEER_TOPK
    a, pos1 = _ranked_top(s1, n)
    b, pos2 = _ranked_top(s2, n)
    amat = jnp.concatenate(a, axis=0)
    bmat = jnp.concatenate(b, axis=0)
    row8 = lax.broadcasted_iota(jnp.int32, (SUBLANES, LANES), 0)
    pieces = [a[0] + bmat, a[1] + bmat[:SUBLANES]]
    for i in range(2, SUBLANES):
        pieces.append(jnp.where(row8 < n // (i + 1), a[i] + bmat[:SUBLANES], NEG))
    pieces.append(amat[SUBLANES:] + b[0])
    comb = jnp.concatenate(pieces, axis=0)
    cur = comb
    for _ in range(n):
        c_n = jnp.max(cur, axis=0, keepdims=True)
        cur = jnp.where(cur >= c_n, NEG, cur)
    cmax = a[0] + b[0]
    z = jnp.sum(jnp.where(comb >= c_n, jnp.exp(comb - cmax), 0.0), axis=0, keepdims=True)
    kb = jnp.zeros((n, LANES), F32)
    for i in range(n):
        kb = kb + jnp.where(a[i] + bmat >= c_n, 1.0, 0.0)
    k2 = jnp.zeros(s2.shape, F32)
    for j in range(n):
        k2 = k2 + jnp.where(pos2 == float(j), kb[j:j + 1], 0.0)
    return pos1, jnp.exp(s1 - a[0]) / z, k2, jnp.exp(s2 - b[0])


def _dup16(x):
    hi = pltpu.bitcast(x.astype(BF16).astype(F32), U32) & jnp.uint32(0xFFFF0000)
    return hi | (hi >> 16)


def _row_bf16(ref, idx, row):
    w = ref[idx + (pl.ds(row, 1), slice(None))]
    return pltpu.bitcast(jnp.broadcast_to(w, (N_KEYS // 2, LANES)), BF16)


def _peer_kernel(h_ref, g_ref, wq_ref, keys_ref, u0_ref, u_ref, vt_ref, vtl_ref, gf_ref, out_ref,
                 tb_scr, q_scr, p1_scr, e1_scr, k2_scr, e2_scr,
                 hid_a, hid_b, coef_a, coef_b, acc_scr, *, final_norm):
    c = pl.program_id(1)
    last = pl.num_programs(1) - 1
    tb_n = h_ref.shape[0]
    nlb = tb_n // LANES
    rows_per_chunk = u_ref.shape[0] // N_KEYS
    tiles = rows_per_chunk // SUBLANES

    def store_hid(dst, hid):
        for lb in range(nlb):
            dst[lb] = hid[:, lb * LANES:(lb + 1) * LANES]

    @pl.when(c == 0)
    def _():
        tb = _rms(h_ref[...], g_ref[...]).astype(BF16)
        tb_scr[...] = tb
        q = jnp.dot(tb, wq_ref[...], preferred_element_type=F32).astype(BF16)
        for hp in range(2 * PEER_HEADS):
            q_scr[hp] = q[:, hp * D_HALF:(hp + 1) * D_HALF]

        def route(h, carry):
            s1 = lax.dot_general(keys_ref[2 * h], q_scr[2 * h], NT_DIMS,
                                 preferred_element_type=F32)
            s2 = lax.dot_general(keys_ref[2 * h + 1], q_scr[2 * h + 1], NT_DIMS,
                                 preferred_element_type=F32)
            for lb in range(nlb):
                sl = slice(lb * LANES, (lb + 1) * LANES)
                pos1, e1, k2, e2 = _route(s1[:, sl], s2[:, sl])
                p1_scr[h, lb] = _dup16(pos1).reshape(p1_scr.shape[2:])
                e1_scr[h, lb] = _dup16(e1).reshape(e1_scr.shape[2:])
                k2_scr[h, lb] = k2.astype(BF16)
                e2_scr[h, lb] = e2.astype(BF16)
            return carry

        lax.fori_loop(0, PEER_HEADS, route, 0)
        store_hid(hid_a, lax.dot_general(u0_ref[...], tb, NT_DIMS, preferred_element_type=F32))
        coef_b[...] = jnp.zeros(coef_b.shape, BF16)
        acc_scr[...] = jnp.zeros(acc_scr.shape, F32)

    def step(hid_cur, hid_nxt, coef_cur, coef_prev):
        ec = u_ref.shape[0]
        d = vt_ref.shape[0]
        pieces = []
        for t0 in range(0, tb_n, MXU_N):
            ts = slice(t0, t0 + MXU_N)
            for part in range(MXU_PARTS):
                dr = slice(part * d // MXU_PARTS, (part + 1) * d // MXU_PARTS)
                er = slice(part * ec // MXU_PARTS, (part + 1) * ec // MXU_PARTS)

                def apply_prev(ts=ts, dr=dr):
                    acc_scr[dr, ts] += jnp.dot(vt_ref[dr, :], coef_prev[:, ts],
                                               preferred_element_type=F32)

                def score_next(ts=ts, er=er, t0=t0):
                    hid = lax.dot_general(u_ref[er, :], tb_scr[ts, :], NT_DIMS,
                                          preferred_element_type=F32)
                    for k in range(MXU_N // LANES):
                        hid_nxt[t0 // LANES + k, er, :] = hid[:, k * LANES:(k + 1) * LANES]

                pieces += [apply_prev, score_next]

        def gate_unit(i1, lb):
            rows = slice(i1 * N_KEYS, (i1 + 1) * N_KEYS)
            gate = None
            for h in range(PEER_HEADS):
                tile = (h, lb, c * tiles + i1 // SUBLANES)
                p1 = _row_bf16(p1_scr, tile, i1 % SUBLANES)
                e1 = _row_bf16(e1_scr, tile, i1 % SUBLANES)
                term = jnp.where(p1 < k2_scr[h, lb], e2_scr[h, lb] * e1,
                                 jnp.zeros((N_KEYS, LANES), BF16))
                gate = term if gate is None else gate + term
            hd = hid_cur[lb, rows, :]
            act = 0.5 * hd * (1.0 + lax.erf(hd * np.float32(math.sqrt(0.5))))
            coef_cur[rows, lb * LANES:(lb + 1) * LANES] = gate * act.astype(BF16)

        units = [(i1, lb) for i1 in range(rows_per_chunk) for lb in range(nlb)]
        per_piece = -(-len(units) // len(pieces))
        for k, piece in enumerate(pieces):
            piece()
            for i1, lb in units[k * per_piece:(k + 1) * per_piece]:
                gate_unit(i1, lb)

    @pl.when(c % 2 == 0)
    def _():
        step(hid_a, hid_b, coef_a, coef_b)

    @pl.when(c % 2 == 1)
    def _():
        step(hid_b, hid_a, coef_b, coef_a)

    @pl.when(c == last)
    def _():
        y = h_ref[...] + (acc_scr[...] + jnp.dot(vtl_ref[...], coef_b[...],
                                                 preferred_element_type=F32)).T
        out_ref[...] = _rms(y, gf_ref[...]) if final_norm else y


def _peer(h, g, wq, keys, u, vt, gf, final_norm, tb=PEER_TOKENS, ec=PEER_CHUNK):
    n, d = h.shape
    n_exp = u.shape[0]
    nlb = tb // LANES
    nc = n_exp // ec
    assert nc % 2 == 0 and ec % (N_KEYS * SUBLANES) == 0 and PEER_TOPK == 2 * SUBLANES
    tiles = ec // (N_KEYS * SUBLANES)
    row = lambda i, c: (i, 0)
    const2 = lambda i, c: (0, 0)
    return pl.pallas_call(
        functools.partial(_peer_kernel, final_norm=final_norm),
        grid=(n // tb, nc),
        in_specs=[pl.BlockSpec((tb, d), row),
                  _resident((1, d), const2),
                  _resident(wq.shape, const2),
                  _resident(keys.shape, lambda i, c: (0, 0, 0)),
                  _resident((ec, d), const2),
                  pl.BlockSpec((ec, d), lambda i, c: (jnp.minimum(c + 1, nc - 1), 0)),
                  pl.BlockSpec((d, ec), lambda i, c: (0, jnp.maximum(c - 1, 0))),
                  _resident((d, ec), lambda i, c: (0, nc - 1)),
                  _resident((1, d), const2)],
        out_specs=pl.BlockSpec((tb, d), row),
        out_shape=jax.ShapeDtypeStruct((n, d), F32),
        scratch_shapes=[
            pltpu.VMEM((tb, d), BF16),
            pltpu.VMEM((2 * PEER_HEADS, tb, D_HALF), BF16),
            pltpu.VMEM((PEER_HEADS, nlb, nc * tiles, SUBLANES, LANES), U32),
            pltpu.VMEM((PEER_HEADS, nlb, nc * tiles, SUBLANES, LANES), U32),
            pltpu.VMEM((PEER_HEADS, nlb, N_KEYS, LANES), BF16),
            pltpu.VMEM((PEER_HEADS, nlb, N_KEYS, LANES), BF16),
            pltpu.VMEM((nlb, ec, LANES), F32),
            pltpu.VMEM((nlb, ec, LANES), F32),
            pltpu.VMEM((ec, tb), BF16),
            pltpu.VMEM((ec, tb), BF16),
            pltpu.VMEM((d, tb), F32),
        ],
        compiler_params=_params("parallel", "arbitrary"),
        name="peer",
    )(h, g, wq, keys, u, u, vt, vt, gf)


def _rope_tables(pos):
    half = HEAD_DIM // 2
    inv = ROPE_THETA ** (-jnp.arange(half, dtype=F32) / half)
    ang = pos.astype(F32)[:, None] * inv[None, :]
    sign = jnp.tile(jnp.concatenate([-jnp.ones((half,), F32), jnp.ones((half,), F32)]),
                    LANES // HEAD_DIM)
    reps = LANES // half
    return jnp.tile(jnp.cos(ang), (1, reps)), jnp.tile(jnp.sin(ang), (1, reps)) * sign[None, :]


def kernel(x_prompt, x_sample, cache_k, cache_v, state_pool, meta_tokens, norm_mix, norm_ffn,
           norm_final, w_qkv, w_o, lambda_q1, lambda_k1, lambda_q2, lambda_k2, subln_gain,
           pool_w, pool_scale, peer_wq, peer_keys, peer_u, peer_v):
    b, seq, d = x_prompt.shape
    sb, ss, _ = x_sample.shape
    past = cache_k.shape[2]
    lp = (seq + N_META + ATT_Q - 1) // ATT_Q * ATT_Q
    row2 = lambda a: a.reshape(1, -1)

    meta = jnp.broadcast_to(meta_tokens[None].astype(x_prompt.dtype), (b, N_META, d))
    hp = jnp.concatenate([x_prompt, meta, jnp.zeros((b, lp - seq - N_META, d), F32)], axis=1)
    hp = hp.reshape(b * lp, d)
    hs = x_sample.reshape(sb * ss, d)

    pos_p = jnp.concatenate([N_META + jnp.arange(seq), jnp.arange(N_META),
                             jnp.zeros((lp - seq - N_META,), jnp.int32)])
    tab_reps = math.lcm(lp, ROW_TILE) // lp
    cos_p, sin_p = (jnp.tile(t, (tab_reps, 1)) for t in _rope_tables(pos_p))
    pos_s = jnp.tile(past + jnp.arange(ss), ROW_TILE // ss)
    cos_s, sin_s = _rope_tables(pos_s)

    peer_w = []
    for i in range(2):
        peer_w.append((row2(norm_ffn[i]), peer_wq[i].astype(BF16),
                       peer_keys[i].reshape(2 * PEER_HEADS, N_KEYS, D_HALF).astype(BF16),
                       peer_u[i].astype(BF16), peer_v[i].T.astype(BF16)))
    gfin = row2(norm_final)

    lam_init = 0.8 - 0.6 * math.exp(-0.3 * 0)
    wqkv = w_qkv[0].astype(BF16)
    wo = w_o[0].astype(BF16)
    gain = row2(subln_gain[0])
    lams = (row2(lambda_q1[0]), row2(lambda_k1[0]), row2(lambda_q2[0]), row2(lambda_k2[0]))
    g0 = row2(norm_mix[0])

    q, k, v, kb, vb = _qkv(hp, g0, wqkv, cos_p, sin_p, ROW_TILE)
    as3 = lambda a: a.reshape(b, lp, -1)
    o = _attn_prompt(as3(q), as3(kb), as3(vb), gain, lams, lam_init, seq // ATT_Q)
    hp = _oproj(o.reshape(b * lp, -1), wo, hp, ROW_TILE)

    qs, ks, vs, kbs, vbs = _qkv(hs, g0, wqkv, cos_s, sin_s, ROW_TILE)
    ss3 = lambda a: a.reshape(sb, ss, -1)
    o_s = _attn_sample(ss3(qs), ss3(kbs), ss3(vbs), cache_k[0].reshape(sb, past, -1),
                       cache_v[0].reshape(sb, past, -1), gain, lams, lam_init)
    hs = _oproj(o_s.reshape(sb * ss, -1), wo, hs, ROW_TILE)

    hp = _peer(hp, *peer_w[0], gfin, False)
    hs = _peer(hs, *peer_w[0], gfin, False)

    g1 = row2(norm_mix[1])
    pw = pool_w[0].astype(BF16)
    psc = row2(pool_scale[0])
    hp3 = hp.reshape(b, lp, d)
    blocks_per_tile = ROW_TILE // HALO
    meta_blk = seq // HALO
    halo_p = lambda bi, i: (bi, jnp.where(i == 0, meta_blk, i * blocks_per_tile - 1), 0)
    hp3, pool_p = _pool(hp3, hp3, halo_p, g1, pw, psc, ROW_TILE, seq // ROW_TILE, False)
    hs3 = hs.reshape(sb, ss, d)
    state = jnp.pad(state_pool[0], ((0, 0), (HALO - POOL_STATE, 0), (0, 0)))
    hs3, pool_s = _pool(hs3, state, lambda bi, i: (bi, 0, 0), g1, pw, psc, ss, 1, True)

    y_p = _peer(hp3.reshape(b * seq, d), *peer_w[1], gfin, True).reshape(b, seq, d)
    y_s = _peer(hs3.reshape(sb * ss, d), *peer_w[1], gfin, True).reshape(sb, ss, d)

    def prompt_cache(a):
        a = a.reshape(b, lp, N_HEADS, -1)
        return jnp.concatenate([a[:, seq:seq + N_META], a[:, :seq]], axis=1)[None]

    return (y_p, y_s, prompt_cache(k), prompt_cache(v), pool_p[:, HALO - POOL_STATE:][None],
            ks.reshape(1, sb, ss, N_HEADS, -1), vs.reshape(1, sb, ss, N_HEADS, -1),
            pool_s[:, HALO - POOL_STATE:][None])
```
